```python
import jax, jax.numpy as jnp
from jax import lax
import numpy as np

D_MODEL = 1024
BATCH = 1
SEQ = 16384
DEPTH = 2
DEC_BATCH = 16
DEC_SEQ = 2048
PAST_LEN = 128

D_MIX = D_MODEL
D_A = D_MIX // 2
D_B = D_MIX - D_A
CHUNK = 128
A_HEADS = 4
A_HEAD_DIM = D_A // A_HEADS
CONV_WIDTH = 31
CONV_PAD = CONV_WIDTH // 2
C_GROUPS = 4
D_FF = ((8 * D_MODEL // 3 + 255) // 256) * 256
N_EVEN = (DEPTH + 1) // 2
EPS = 1e-6

kernel_name = "hybrid_sgmlp_conv_fnet_encoder"


def rms_norm(x, g):
    xf = x.astype(jnp.float32)
    y = xf * lax.rsqrt(jnp.mean(xf * xf, axis=-1, keepdims=True) + EPS)
    return (y * g.astype(jnp.float32)).astype(x.dtype)


def layer_norm(x, g, b):
    xf = x.astype(jnp.float32)
    mu = jnp.mean(xf, axis=-1, keepdims=True)
    xc = xf - mu
    var = jnp.mean(xc * xc, axis=-1, keepdims=True)
    y = xc * lax.rsqrt(var + EPS) * g.astype(jnp.float32) + b.astype(jnp.float32)
    return y.astype(x.dtype)


def swiglu(x, w_gate, w_up, w_down):
    return (jax.nn.silu(x @ w_gate) * (x @ w_up)) @ w_down


def mixer_ab(h, w_in, sg_ln_g, sg_ln_b, sg_w, sg_b, conv_w, conv_b, conv_ln_g, conv_ln_b, w_out):
    bsz, s, _ = h.shape
    proj = h @ w_in
    z = jax.nn.gelu(proj[..., :2 * D_A])
    u, v = z[..., :D_A], z[..., D_A:]
    v = layer_norm(v, sg_ln_g, sg_ln_b)
    v = v.reshape(bsz, s // CHUNK, CHUNK, A_HEADS, A_HEAD_DIM)
    sv = jnp.einsum('hqp,bnphc->bnqhc', sg_w, v) + sg_b.T[None, None, :, :, None]
    a_out = u * sv.reshape(bsz, s, D_A)
    g_in = proj[..., 2 * D_A:]
    c = g_in[..., :D_B] * jax.nn.sigmoid(g_in[..., D_B:])
    c = lax.conv_general_dilated(
        c, conv_w[:, None, :].astype(c.dtype), window_strides=(1,),
        padding=[(CONV_PAD, CONV_PAD)], dimension_numbers=('NWC', 'WIO', 'NWC'),
        feature_group_count=D_B) + conv_b
    c = jax.nn.silu(layer_norm(c, conv_ln_g, conv_ln_b))
    return jnp.concatenate([a_out, c], axis=-1) @ w_out


def mixer_c(h, w_out):
    bsz, s, d = h.shape
    hf = h.astype(jnp.float32).reshape(bsz, s, C_GROUPS, d // C_GROUPS)
    f = jnp.fft.fftn(hf, axes=(1, 3), norm='ortho').real.astype(h.dtype).reshape(bsz, s, d)
    return f @ w_out


def half_ffn(x, pre_g, w_gate, w_up, w_down, post_g):
    return x + 0.5 * rms_norm(swiglu(rms_norm(x, pre_g), w_gate, w_up, w_down), post_g)


def trunk(x, ffn_pre_g, ffn_w_gate, ffn_w_up, ffn_w_down, ffn_post_g,
          mix_pre_g, mix_w_out, mix_post_g, ab_w_in, sg_ln_g, sg_ln_b, sg_w, sg_b,
          conv_w, conv_b, conv_ln_g, conv_ln_b):
    for i in range(DEPTH):
        x = half_ffn(x, ffn_pre_g[i, 0], ffn_w_gate[i, 0], ffn_w_up[i, 0], ffn_w_down[i, 0], ffn_post_g[i, 0])
        h = rms_norm(x, mix_pre_g[i])
        if i % 2 == 0:
            e = i // 2
            m = mixer_ab(h, ab_w_in[e], sg_ln_g[e], sg_ln_b[e], sg_w[e], sg_b[e],
                         conv_w[e], conv_b[e], conv_ln_g[e], conv_ln_b[e], mix_w_out[i])
        else:
            m = mixer_c(h, mix_w_out[i])
        x = x + rms_norm(m, mix_post_g[i])
        x = half_ffn(x, ffn_pre_g[i, 1], ffn_w_gate[i, 1], ffn_w_up[i, 1], ffn_w_down[i, 1], ffn_post_g[i, 1])
    return x


def setup_inputs(seed: int = 0) -> dict:
    key = jax.random.key(seed)
    ks = jax.random.split(key, 19)
    f32 = jnp.float32

    def nrm(k, shape, scale):
        return jax.random.normal(k, shape, f32) * scale

    def gain(k, shape):
        return 1.0 + 0.02 * jax.random.normal(k, shape, f32)

    return {
        "x_prompt": nrm(ks[0], (BATCH, SEQ, D_MODEL), 1.0),
        "x_sample": nrm(ks[1], (DEC_BATCH, DEC_SEQ, D_MODEL), 1.0),
        "ffn_pre_g": gain(ks[2], (DEPTH, 2, D_MODEL)),
        "ffn_w_gate": nrm(ks[3], (DEPTH, 2, D_MODEL, D_FF), D_MODEL ** -0.5),
        "ffn_w_up": nrm(ks[4], (DEPTH, 2, D_MODEL, D_FF), D_MODEL ** -0.5),
        "ffn_w_down": nrm(ks[5], (DEPTH, 2, D_FF, D_MODEL), D_FF ** -0.5),
        "ffn_post_g": gain(ks[6], (DEPTH, 2, D_MODEL)),
        "mix_pre_g": gain(ks[7], (DEPTH, D_MODEL)),
        "mix_w_out": nrm(ks[8], (DEPTH, D_MIX, D_MODEL), D_MIX ** -0.5),
        "mix_post_g": gain(ks[9], (DEPTH, D_MODEL)),
        "ab_w_in": nrm(ks[10], (N_EVEN, D_MODEL, 2 * D_A + 2 * D_B), D_MODEL ** -0.5),
        "sg_ln_g": gain(ks[11], (N_EVEN, D_A)),
        "sg_ln_b": nrm(ks[12], (N_EVEN, D_A), 0.02),
        "sg_w": nrm(ks[13], (N_EVEN, A_HEADS, CHUNK, CHUNK), CHUNK ** -0.5),
        "sg_b": gain(ks[14], (N_EVEN, A_HEADS, CHUNK)),
        "conv_w": nrm(ks[15], (N_EVEN, CONV_WIDTH, D_B), CONV_WIDTH ** -0.5),
        "conv_b": nrm(ks[16], (N_EVEN, D_B), 0.02),
        "conv_ln_g": gain(ks[17], (N_EVEN, D_B)),
        "conv_ln_b": nrm(ks[18], (N_EVEN, D_B), 0.02),
    }


def reference(x_prompt, x_sample, ffn_pre_g, ffn_w_gate, ffn_w_up, ffn_w_down, ffn_post_g,
              mix_pre_g, mix_w_out, mix_post_g, ab_w_in, sg_ln_g, sg_ln_b, sg_w, sg_b,
              conv_w, conv_b, conv_ln_g, conv_ln_b):
    params = (ffn_pre_g, ffn_w_gate, ffn_w_up, ffn_w_down, ffn_post_g,
              mix_pre_g, mix_w_out, mix_post_g, ab_w_in, sg_ln_g, sg_ln_b, sg_w, sg_b,
              conv_w, conv_b, conv_ln_g, conv_ln_b)
    y_prompt = trunk(x_prompt, *params)
    y_sample = trunk(x_sample, *params)
    return (y_prompt, y_sample)
```

```python
import functools

import numpy as np
import jax
import jax.numpy as jnp
from jax import lax
from jax.experimental import pallas as pl
from jax.experimental.pallas import tpu as pltpu

EPS = 1e-6
F32 = jnp.float32
BF16 = jnp.bfloat16

VMEM_LIMIT_BYTES = 56 * 1024 * 1024

CHUNK = 128
A_HEADS = 4
CONV_WIDTH = 31
CONV_PAD = CONV_WIDTH // 2
HALO = 16
CONV_ROWS = 32
SUBLANES = 8
C_GROUPS = 4
DFT_N2 = 128
DFT_ROWS = 128


def _rms(x, g):
    ms = jnp.mean(x * x, axis=-1, keepdims=True)
    return x * lax.rsqrt(ms + EPS) * g


def _layer_norm(x, g, b):
    mu = jnp.mean(x, axis=-1, keepdims=True)
    xc = x - mu
    var = jnp.mean(xc * xc, axis=-1, keepdims=True)
    return xc * lax.rsqrt(var + EPS) * g + b


def _gelu_tanh(x):
    c = np.float32(np.sqrt(2.0 / np.pi))
    return 0.5 * x * (1.0 + jnp.tanh(c * (x + 0.044715 * (x * x * x))))


def _silu(x):
    return x * jax.nn.sigmoid(x)


def _resident(shape):
    nd = len(shape)
    return pl.BlockSpec(shape, lambda i: (0,) * nd, pipeline_mode=pl.Buffered(1))


def _params():
    return pltpu.CompilerParams(
        dimension_semantics=("arbitrary",), vmem_limit_bytes=VMEM_LIMIT_BYTES)


def _ffn_kernel(x_ref, pre_g_ref, wg_ref, wu_ref, wd_ref, post_g_ref, o_ref):
    x = x_ref[...]
    h = _rms(x, pre_g_ref[...]).astype(BF16)
    g = jnp.dot(h, wg_ref[...], preferred_element_type=F32)
    u = jnp.dot(h, wu_ref[...], preferred_element_type=F32)
    a = (_silu(g) * u).astype(BF16)
    y = jnp.dot(a, wd_ref[...], preferred_element_type=F32)
    o_ref[...] = x + 0.5 * _rms(y, post_g_ref[...])


def _ffn(x, pre_g, wg, wu, wd, post_g, *, tm):
    n, d = x.shape
    f = wg.shape[1]
    return pl.pallas_call(
        _ffn_kernel,
        grid=(n // tm,),
        in_specs=[
            pl.BlockSpec((tm, d), lambda i: (i, 0)),
            _resident((1, d)),
            _resident((d, f)),
            _resident((d, f)),
            _resident((f, d)),
            _resident((1, d)),
        ],
        out_specs=pl.BlockSpec((tm, d), lambda i: (i, 0)),
        out_shape=jax.ShapeDtypeStruct((n, d), F32),
        compiler_params=_params(),
        name="half_ffn",
    )(x, pre_g, wg, wu, wd, post_g)


def _mixab_kernel(xm_ref, xp_ref, xn_ref, pre_g_ref, w_sgu_ref, w_glu_ref,
                  sg_ln_g_ref, sg_ln_b_ref, sgw_ref, sgb_ref,
                  conv_w_ref, conv_b_ref, conv_ln_g_ref, conv_ln_b_ref,
                  w_out_ref, post_g_ref, o_ref, hbuf, cbuf, abuf,
                  *, tm, tiles_per_seq):
    d_a = sg_ln_g_ref.shape[1]
    d_b = conv_b_ref.shape[1]
    head_dim = d_a // A_HEADS
    i = pl.program_id(0)
    first = (i % tiles_per_seq) == 0
    last = (i % tiles_per_seq) == (tiles_per_seq - 1)

    x = xm_ref[...]
    pre_g = pre_g_ref[...]
    hbuf[0:HALO, :] = _rms(xp_ref[...], pre_g).astype(BF16)
    hbuf[HALO:HALO + tm, :] = _rms(x, pre_g).astype(BF16)
    hbuf[HALO + tm:, :] = _rms(xn_ref[...], pre_g).astype(BF16)

    glu = jnp.dot(hbuf[...], w_glu_ref[...], preferred_element_type=F32)
    c = glu[:, :d_b] * jax.nn.sigmoid(glu[:, d_b:])
    row = lax.broadcasted_iota(jnp.int32, (tm + 2 * HALO, 1), 0)
    lo = jnp.where(first, HALO, 0)
    hi = jnp.where(last, tm + HALO, tm + 2 * HALO)
    cbuf[0] = jnp.where((row >= lo) & (row < hi), c, 0.0)
    n_shift = tm + 2 * HALO - SUBLANES
    for r in range(1, SUBLANES):
        cbuf[r, 0:n_shift, :] = cbuf[0, r:r + n_shift, :]

    z = _gelu_tanh(jnp.dot(hbuf[HALO:HALO + tm, :], w_sgu_ref[...], preferred_element_type=F32))
    u = z[:, :d_a]
    v = _layer_norm(z[:, d_a:], sg_ln_g_ref[...], sg_ln_b_ref[...]).astype(BF16)
    blk = 2 * CHUNK
    for r in range(tm // blk):
        for h in range(A_HEADS):
            cols = slice(h * head_dim, (h + 1) * head_dim)
            sv = jnp.dot(sgw_ref[h], v[r * blk:(r + 1) * blk, cols], preferred_element_type=F32)
            sv = sv + sgb_ref[h]
            abuf[r * blk:(r + 1) * blk, cols] = (u[r * blk:(r + 1) * blk, cols] * sv).astype(BF16)

    conv_b = conv_b_ref[...]
    ln_g = conv_ln_g_ref[...]
    ln_b = conv_ln_b_ref[...]

    def conv_block(k, carry):
        r0 = pl.multiple_of(k * CONV_ROWS, CONV_ROWS)
        acc = jnp.zeros((CONV_ROWS, d_b), F32)
        for j in range(CONV_WIDTH):
            off = HALO - CONV_PAD + j
            q, r = divmod(off, SUBLANES)
            acc = acc + conv_w_ref[j:j + 1, :] * cbuf[r, pl.ds(r0 + q * SUBLANES, CONV_ROWS), :]
        y = _silu(_layer_norm(acc + conv_b, ln_g, ln_b))
        abuf[pl.ds(r0, CONV_ROWS), d_a:] = y.astype(BF16)
        return carry

    lax.fori_loop(0, tm // CONV_ROWS, conv_block, 0)

    m = jnp.dot(abuf[...], w_out_ref[...], preferred_element_type=F32)
    o_ref[...] = x + _rms(m, post_g_ref[...])


def _mixer_ab(x, seq_len, pre_g, w_sgu, w_glu, sg_ln_g, sg_ln_b, sgw_bd, sgb_bc,
              conv_w, conv_b, conv_ln_g, conv_ln_b, w_out, post_g, *, tm):
    n, d = x.shape
    d_a = sg_ln_g.shape[1]
    d_b = conv_b.shape[1]
    tiles_per_seq = seq_len // tm
    halo_blocks = n // HALO
    per_tile = tm // HALO
    kern = functools.partial(_mixab_kernel, tm=tm, tiles_per_seq=tiles_per_seq)
    return pl.pallas_call(
        kern,
        grid=(n // tm,),
        in_specs=[
            pl.BlockSpec((tm, d), lambda i: (i, 0)),
            pl.BlockSpec((HALO, d), lambda i: (jnp.maximum(i * per_tile - 1, 0), 0)),
            pl.BlockSpec((HALO, d), lambda i: (jnp.minimum((i + 1) * per_tile, halo_blocks - 1), 0)),
            _resident((1, d)),
            _resident(w_sgu.shape),
            _resident(w_glu.shape),
            _resident((1, d_a)),
            _resident((1, d_a)),
            _resident(sgw_bd.shape),
            _resident(sgb_bc.shape),
            _resident(conv_w.shape),
            _resident((1, d_b)),
            _resident((1, d_b)),
            _resident((1, d_b)),
            _resident(w_out.shape),
            _resident((1, d)),
        ],
        out_specs=pl.BlockSpec((tm, d), lambda i: (i, 0)),
        out_shape=jax.ShapeDtypeStruct((n, d), F32),
        scratch_shapes=[
            pltpu.VMEM((tm + 2 * HALO, d), BF16),
            pltpu.VMEM((SUBLANES, tm + 2 * HALO, d_b), F32),
            pltpu.VMEM((tm, d_a + d_b), BF16),
        ],
        compiler_params=_params(),
        name="mixer_ab",
    )(x, x, x, pre_g, w_sgu, w_glu, sg_ln_g, sg_ln_b, sgw_bd, sgb_bc,
      conv_w, conv_b, conv_ln_g, conv_ln_b, w_out, post_g)


def _fnet_s1_kernel(x_ref, pre_g_ref, wc_ref, g_ref, o_ref, xbuf, hbuf, aabuf, *, nt):
    rows = x_ref.shape[0]
    d = pre_g_ref.shape[1]
    gw = d // C_GROUPS
    for t in range(nt):
        xbuf[t * rows:(t + 1) * rows, :] = x_ref[:, t, :]
    hbuf[...] = _rms(xbuf[...], pre_g_ref[...]).astype(BF16)
    for g in range(C_GROUPS):
        a = jnp.dot(hbuf[:, g * gw:(g + 1) * gw], wc_ref[...], preferred_element_type=F32)
        for t in range(nt):
            aabuf[t, 0:rows, g * gw:(g + 1) * gw] = a[t * rows:(t + 1) * rows, :gw].astype(BF16)
            aabuf[t, rows:2 * rows, g * gw:(g + 1) * gw] = a[t * rows:(t + 1) * rows, gw:].astype(BF16)
    for t in range(nt):
        o_ref[t] = jnp.dot(g_ref[t], aabuf[t], preferred_element_type=F32)


def _fnet_s1(xv, pre_g, wc, g_tab, *, nt):
    rows_total, n2, d = xv.shape
    units = rows_total // DFT_ROWS
    steps = n2 // nt
    kern = functools.partial(_fnet_s1_kernel, nt=nt)
    return pl.pallas_call(
        kern,
        grid=(units * steps,),
        in_specs=[
            pl.BlockSpec((DFT_ROWS, nt, d), lambda i: (i // steps, i % steps, 0)),
            _resident((1, d)),
            _resident(wc.shape),
            pl.BlockSpec((nt, 2 * DFT_ROWS, 2 * DFT_ROWS), lambda i: (i % steps, 0, 0)),
        ],
        out_specs=pl.BlockSpec((None, nt, 2 * DFT_ROWS, d), lambda i: (i // steps, i % steps, 0, 0)),
        out_shape=jax.ShapeDtypeStruct((units, n2, 2 * DFT_ROWS, d), F32),
        scratch_shapes=[
            pltpu.VMEM((nt * DFT_ROWS, d), F32),
            pltpu.VMEM((nt * DFT_ROWS, d), BF16),
            pltpu.VMEM((nt, 2 * DFT_ROWS, d), BF16),
        ],
        compiler_params=_params(),
        name="fnet_s1",
    )(xv, pre_g, wc, g_tab)


def _fnet_s2_kernel(bre_ref, bim_ref, x_ref, f2_ref, w_out_ref, post_g_ref, o_ref, rbuf, ybuf, xbuf, *, kt):
    n2 = bre_ref.shape[0]
    for t in range(kt):
        s = t % rbuf.shape[0]
        rbuf[s, 0:n2, :] = bre_ref[:, t, :]
        rbuf[s, n2:2 * n2, :] = bim_ref[:, t, :]
        xbuf[t * n2:(t + 1) * n2, :] = x_ref[:, t, :]
        y = jnp.dot(f2_ref[...], rbuf[s].astype(BF16), preferred_element_type=F32)
        ybuf[t * n2:(t + 1) * n2, :] = y.astype(BF16)
    m = jnp.dot(ybuf[...], w_out_ref[...], preferred_element_type=F32)
    xbuf[...] = xbuf[...] + _rms(m, post_g_ref[...])
    for t in range(kt):
        o_ref[:, t, :] = xbuf[t * n2:(t + 1) * n2, :]


def _fnet_s2(bp, xv, f2, w_out, post_g, *, kt):
    units, n2, rows2, d = bp.shape
    n1 = xv.shape[1]
    row_steps = DFT_ROWS // kt
    col_steps = n1 // kt
    kern = functools.partial(_fnet_s2_kernel, kt=kt)
    x_spec = pl.BlockSpec((n2, kt, d), lambda i: (i // col_steps, i % col_steps, 0))
    return pl.pallas_call(
        kern,
        grid=(units * row_steps,),
        in_specs=[
            pl.BlockSpec((None, n2, kt, d), lambda i: (i // row_steps, 0, i % row_steps, 0)),
            pl.BlockSpec((None, n2, kt, d), lambda i: (i // row_steps, 0, row_steps + i % row_steps, 0)),
            x_spec,
            _resident(f2.shape),
            _resident(w_out.shape),
            _resident((1, d)),
        ],
        out_specs=x_spec,
        out_shape=jax.ShapeDtypeStruct(xv.shape, F32),
        scratch_shapes=[
            pltpu.VMEM((2, 2 * n2, d), F32),
            pltpu.VMEM((kt * n2, d), BF16),
            pltpu.VMEM((kt * n2, d), F32),
        ],
        compiler_params=_params(),
        name="fnet_s2",
    )(bp, bp, xv, f2, w_out, post_g)


def _dft_tables(seq_len, d):
    gw = d // C_GROUPS
    n1 = seq_len // DFT_N2
    ang = 2.0 * np.pi * np.outer(np.arange(gw), np.arange(gw)) / gw
    wc = np.concatenate([np.cos(ang), -np.sin(ang)], axis=1) / np.sqrt(gw)
    ang2 = 2.0 * np.pi * np.outer(np.arange(DFT_N2), np.arange(DFT_N2)) / DFT_N2
    f2 = np.concatenate([np.cos(ang2), np.sin(ang2)], axis=1) / np.sqrt(DFT_N2)
    ang1 = 2.0 * np.pi * np.outer(np.arange(n1), np.arange(n1)) / n1
    c1 = jnp.asarray(np.cos(ang1) / np.sqrt(n1), F32)[None]
    s1 = jnp.asarray(np.sin(ang1) / np.sqrt(n1), F32)[None]
    angt = 2.0 * np.pi * np.outer(np.arange(DFT_N2), np.arange(n1)) / seq_len
    tc = jnp.asarray(np.cos(angt), F32)[:, :, None]
    ts = jnp.asarray(np.sin(angt), F32)[:, :, None]
    gc = tc * c1 - ts * s1
    gs = ts * c1 + tc * s1
    blk = jnp.stack([jnp.stack([gc, gs], axis=2), jnp.stack([-gs, gc], axis=2)], axis=1)
    seqs = DFT_ROWS // n1
    eye = jnp.eye(seqs, dtype=F32)
    g = eye[None, None, :, None, None, :, None] * blk[:, :, None, :, :, None, :]
    g = g.reshape(DFT_N2, 2 * DFT_ROWS, 2 * DFT_ROWS)
    return (jnp.asarray(wc, F32).astype(BF16), jnp.asarray(f2, F32).astype(BF16), g.astype(BF16))


def _mixer_c(x3, pre_g, w_out, post_g):
    bsz, s, d = x3.shape
    n1 = s // DFT_N2
    wc, f2, g_tab = _dft_tables(s, d)
    bp = _fnet_s1(x3.reshape(bsz * n1, DFT_N2, d), pre_g, wc, g_tab, nt=8)
    out = _fnet_s2(bp, x3.reshape(bsz * DFT_N2, n1, d), f2, w_out, post_g, kt=8)
    return out.reshape(bsz, s, d)


def _trunk(x3, p):
    bsz, s, d = x3.shape
    depth = p["ffn_pre_g"].shape[0]
    x = x3.reshape(bsz * s, d)
    for i in range(depth):
        x = _ffn(x, p["ffn_pre_g"][i, 0][None], p["wg"][i][0], p["wu"][i][0], p["wd"][i][0],
                 p["ffn_post_g"][i, 0][None], tm=512)
        if i % 2 == 0:
            e = i // 2
            x = _mixer_ab(x, s, p["mix_pre_g"][i][None], p["w_sgu"][e], p["w_glu"][e],
                          p["sg_ln_g"][e][None], p["sg_ln_b"][e][None], p["sgw_bd"][e], p["sgb_bc"][e],
                          p["conv_w"][e], p["conv_b"][e][None], p["conv_ln_g"][e][None],
                          p["conv_ln_b"][e][None], p["w_out"][i], p["mix_post_g"][i][None], tm=512)
        else:
            x = _mixer_c(x.reshape(bsz, s, d), p["mix_pre_g"][i][None], p["w_out"][i],
                         p["mix_post_g"][i][None]).reshape(bsz * s, d)
        x = _ffn(x, p["ffn_pre_g"][i, 1][None], p["wg"][i][1], p["wu"][i][1], p["wd"][i][1],
                 p["ffn_post_g"][i, 1][None], tm=512)
    return x.reshape(bsz, s, d)


def kernel(x_prompt, x_sample, ffn_pre_g, ffn_w_gate, ffn_w_up, ffn_w_down, ffn_post_g, mix_pre_g, mix_w_out, mix_post_g, ab_w_in, sg_ln_g, sg_ln_b, sg_w, sg_b, conv_w, conv_b, conv_ln_g, conv_ln_b):
    d_a = sg_ln_g.shape[1]
    n_even = ab_w_in.shape[0]
    eye2 = jnp.eye(2, dtype=F32)
    sgw_bd = (eye2[None, None, :, None, :, None] * sg_w[:, :, None, :, None, :]).reshape(
        n_even, A_HEADS, 2 * CHUNK, 2 * CHUNK)
    sgb_bc = jnp.broadcast_to(jnp.tile(sg_b, (1, 1, 2))[..., None], (n_even, A_HEADS, 2 * CHUNK, d_a // A_HEADS))
    p = dict(
        ffn_pre_g=ffn_pre_g, ffn_post_g=ffn_post_g,
        wg=ffn_w_gate.astype(BF16), wu=ffn_w_up.astype(BF16), wd=ffn_w_down.astype(BF16),
        mix_pre_g=mix_pre_g, mix_post_g=mix_post_g, w_out=mix_w_out.astype(BF16),
        w_sgu=ab_w_in[:, :, :2 * d_a].astype(BF16), w_glu=ab_w_in[:, :, 2 * d_a:].astype(BF16),
        sg_ln_g=sg_ln_g, sg_ln_b=sg_ln_b, sgw_bd=sgw_bd.astype(BF16), sgb_bc=sgb_bc.astype(F32),
        conv_w=conv_w, conv_b=conv_b, conv_ln_g=conv_ln_g, conv_ln_b=conv_ln_b,
    )
    return (_trunk(x_prompt, p), _trunk(x_sample, p))
```
